```python
import jax, jax.numpy as jnp
from jax import lax
import numpy as np

D_MODEL = 1024
BATCH = 2
SEQ = 8192
DEPTH = 2

CTX_LEN = 256
GRID_W = 64
W_CONV = D_MODEL
W_LRU = D_MODEL
LRU_BLOCKS = 8
LRU_BW = W_LRU // LRU_BLOCKS
CONV_A_WIDTH = 3
CONV_B_WIDTH = 4
LRU_C = 8.0
RMS_EPS = 1e-6
PROJ_WIDTHS = (W_CONV, W_CONV, W_CONV, W_CONV, W_LRU, W_LRU, D_MODEL, D_MODEL)
SPLIT_POINTS = tuple(int(v) for v in np.cumsum(PROJ_WIDTHS)[:-1])
D_IN = int(sum(PROJ_WIDTHS))

kernel_name = "hybrid_shortconv_rglru_dit_block"


def rmsnorm(x, g):
    xf = x.astype(jnp.float32)
    y = xf * lax.rsqrt(jnp.mean(xf * xf, axis=-1, keepdims=True) + RMS_EPS)
    return (y * g.astype(jnp.float32)).astype(x.dtype)


def dwconv(v, w, pad_left):
    k = w.shape[0]
    n = v.shape[-2]
    pad = [(0, 0)] * (v.ndim - 2) + [(pad_left, k - 1 - pad_left), (0, 0)]
    vp = jnp.pad(v, pad)
    out = w[0] * lax.slice_in_dim(vp, 0, n, axis=v.ndim - 2)
    for j in range(1, k):
        out = out + w[j] * lax.slice_in_dim(vp, j, j + n, axis=v.ndim - 2)
    return out


def grid_conv(v, w, pad_left):
    b, n, ch = v.shape
    rows = n // GRID_W
    return dwconv(v.reshape(b, rows, GRID_W, ch), w, pad_left).reshape(b, n, ch)


def block_diag(x, w, bias):
    xb = x.reshape(x.shape[:-1] + (LRU_BLOCKS, LRU_BW))
    y = jnp.einsum("blnk,nkj->blnj", xb, w)
    return y.reshape(x.shape) + bias


def _combine(lhs, rhs):
    a1, b1 = lhs
    a2, b2 = rhs
    return a1 * a2, a2 * b1 + b2


def linear_scan(a, u, h0, reverse):
    if reverse:
        a = jnp.flip(a, axis=1)
        u = jnp.flip(u, axis=1)
    u = u.at[:, 0].add(a[:, 0] * h0)
    _, h = lax.associative_scan(_combine, (a, u), axis=1)
    if reverse:
        h = jnp.flip(h, axis=1)
    return h


def rglru_direction(xc, h0, wr, br, wi, bi, lam, reverse):
    r = jax.nn.sigmoid(block_diag(xc, wr, br).astype(jnp.float32))
    i = jax.nn.sigmoid(block_diag(xc, wi, bi).astype(jnp.float32))
    log_a = -LRU_C * r * jax.nn.softplus(-lam.astype(jnp.float32))
    a = jnp.exp(log_a)
    mult = jnp.sqrt(-jnp.expm1(2.0 * log_a))
    u = mult * i * xc.astype(jnp.float32)
    h = linear_scan(a, u, h0, reverse)
    final = h[:, 0] if reverse else h[:, -1]
    return h, final


def bidir_rglru(xc, h0s, wr, br, wi, bi, lam):
    h_f, fin_f = rglru_direction(xc, h0s[0], wr[0], br[0], wi[0], bi[0], lam[0], False)
    h_b, fin_b = rglru_direction(xc, h0s[1], wr[1], br[1], wi[1], bi[1], lam[1], True)
    return (h_f + h_b).astype(xc.dtype), (fin_f, fin_b)


def gated_merge(chunks, y_lru, conv_fn, conv3_w, w_out_a, w_out_b, w_o):
    v, g_b, g_c, z_a, _, z_b, m_a, m_b = chunks
    y_a = g_b * conv_fn(g_c * v, conv3_w, 1) * jax.nn.silu(z_a)
    y_b = y_lru * jax.nn.silu(z_b)
    merged = jax.nn.sigmoid(m_a) * (y_a @ w_out_a) + jax.nn.sigmoid(m_b) * (y_b @ w_out_b)
    return merged @ w_o


def setup_inputs(seed: int = 0) -> dict:
    key = jax.random.key(seed)
    ks = jax.random.split(key, 20)
    f32 = jnp.float32
    nrm = lambda k, shape, s: jax.random.normal(k, shape, f32) * s
    x = nrm(ks[0], (BATCH, SEQ, D_MODEL), 1.0)
    c = nrm(ks[1], (BATCH, D_MODEL), 1.0)
    ctx = nrm(ks[2], (BATCH, CTX_LEN, D_MODEL), 1.0)
    c_ctx = nrm(ks[3], (D_MODEL,), 1.0)
    w_ada = nrm(ks[4], (DEPTH, D_MODEL, 3 * D_MODEL), D_MODEL ** -0.5)
    b_ada = nrm(ks[5], (DEPTH, 3 * D_MODEL), 0.02)
    norm_g = 1.0 + nrm(ks[6], (DEPTH, D_MODEL), 0.05)
    w_in = nrm(ks[7], (DEPTH, D_MODEL, D_IN), D_MODEL ** -0.5)
    conv3_w = nrm(ks[8], (DEPTH, CONV_A_WIDTH, W_CONV), CONV_A_WIDTH ** -0.5)
    conv4_w = nrm(ks[9], (DEPTH, CONV_B_WIDTH, W_LRU), CONV_B_WIDTH ** -0.5)
    conv4_b = nrm(ks[10], (DEPTH, W_LRU), 0.02)
    lru_wr = nrm(ks[11], (DEPTH, 2, LRU_BLOCKS, LRU_BW, LRU_BW), LRU_BW ** -0.5)
    lru_br = nrm(ks[12], (DEPTH, 2, W_LRU), 0.02)
    lru_wi = nrm(ks[13], (DEPTH, 2, LRU_BLOCKS, LRU_BW, LRU_BW), LRU_BW ** -0.5)
    lru_bi = nrm(ks[14], (DEPTH, 2, W_LRU), 0.02)
    a_pow = jax.random.uniform(ks[15], (DEPTH, 2, W_LRU), f32, 0.9, 0.999)
    base = a_pow ** (1.0 / LRU_C)
    lru_lambda = jnp.log(base) - jnp.log1p(-base)
    w_out_a = nrm(ks[16], (DEPTH, W_CONV, D_MODEL), W_CONV ** -0.5)
    w_out_b = nrm(ks[17], (DEPTH, W_LRU, D_MODEL), W_LRU ** -0.5)
    w_o = nrm(ks[18], (DEPTH, D_MODEL, D_MODEL), D_MODEL ** -0.5)
    final_g = 1.0 + nrm(ks[19], (D_MODEL,), 0.05)
    return {"x": x, "c": c, "ctx": ctx, "c_ctx": c_ctx, "w_ada": w_ada, "b_ada": b_ada,
            "norm_g": norm_g, "w_in": w_in, "conv3_w": conv3_w, "conv4_w": conv4_w,
            "conv4_b": conv4_b, "lru_wr": lru_wr, "lru_br": lru_br, "lru_wi": lru_wi,
            "lru_bi": lru_bi, "lru_lambda": lru_lambda, "w_out_a": w_out_a,
            "w_out_b": w_out_b, "w_o": w_o, "final_g": final_g}


def reference(x, c, ctx, c_ctx, w_ada, b_ada, norm_g, w_in, conv3_w, conv4_w, conv4_b,
              lru_wr, lru_br, lru_wi, lru_bi, lru_lambda, w_out_a, w_out_b, w_o, final_g):
    n_batch = x.shape[0]
    silu_c = jax.nn.silu(c)
    silu_cc = jax.nn.silu(c_ctx)
    for l in range(DEPTH):
        last = l == DEPTH - 1
        shift, scale, gate = jnp.split(silu_c @ w_ada[l] + b_ada[l], 3, axis=-1)
        shift_c, scale_c, gate_c = jnp.split(silu_cc @ w_ada[l] + b_ada[l], 3, axis=-1)
        h = rmsnorm(x, norm_g[l]) * (1.0 + scale[:, None]) + shift[:, None]
        hc = rmsnorm(ctx, norm_g[l]) * (1.0 + scale_c) + shift_c
        chunks = jnp.split(h @ w_in[l], SPLIT_POINTS, axis=-1)
        chunks_c = jnp.split(hc @ w_in[l], SPLIT_POINTS, axis=-1)
        lru_p = (lru_wr[l], lru_br[l], lru_wi[l], lru_bi[l], lru_lambda[l])
        xc_c = dwconv(chunks_c[4], conv4_w[l], 2) + conv4_b[l]
        zeros = jnp.zeros((n_batch, W_LRU), jnp.float32)
        y_lru_c, finals = bidir_rglru(xc_c, (zeros, zeros), *lru_p)
        xc = grid_conv(chunks[4], conv4_w[l], 2) + conv4_b[l]
        y_lru, _ = bidir_rglru(xc, finals, *lru_p)
        x = x + gate[:, None] * gated_merge(chunks, y_lru, grid_conv, conv3_w[l],
                                            w_out_a[l], w_out_b[l], w_o[l])
        if not last:
            ctx = ctx + gate_c * gated_merge(chunks_c, y_lru_c, dwconv, conv3_w[l],
                                             w_out_a[l], w_out_b[l], w_o[l])
    return rmsnorm(x, final_g)
```

```python
import functools

import jax
import jax.numpy as jnp
from jax import lax
from jax.experimental import pallas as pl
from jax.experimental.pallas import tpu as pltpu

GRID_W = 64
LRU_BLOCKS = 8
LRU_C = 8.0
RMS_EPS = 1e-6
N_PROJ = 8
CONV_B_PAD_LEFT = 2
CONV_A_PAD_LEFT = 1

SUBLANES = 8
LATENT_TILE = SUBLANES * GRID_W
COL_BLOCK = 256
VMEM_LIMIT_BYTES = 60 * 1024 * 1024

F32 = jnp.float32
BF16 = jnp.bfloat16


def _sigmoid(z):
    return 0.5 + 0.5 * jnp.tanh(0.5 * z)


def _silu(z):
    return z * _sigmoid(z)


def _mod_norm(x, g, shift, scale):
    ms = jnp.mean(x * x, axis=-1, keepdims=True)
    return (x * lax.rsqrt(ms + RMS_EPS) * g) * (1.0 + scale) + shift


def _token_shift(v, d, wrap):
    n = abs(d)
    if n == 0:
        return v
    rows, width = v.shape
    slabs = rows // SUBLANES
    if not wrap:
        edge = [jnp.zeros((n * SUBLANES, width), v.dtype)]
    else:
        sub = lax.broadcasted_iota(jnp.int32, (SUBLANES, width), 0)
        edge = []
        for i in range(n):
            if d < 0:
                src = (slabs - n + i) * SUBLANES
                slab = pltpu.roll(v[src:src + SUBLANES], 1, 0)
                edge.append(jnp.where(sub >= 1, slab, 0.0))
            else:
                src = i * SUBLANES
                slab = pltpu.roll(v[src:src + SUBLANES], SUBLANES - 1, 0)
                edge.append(jnp.where(sub <= SUBLANES - 2, slab, 0.0))
    if d < 0:
        return jnp.concatenate(edge + [v[:rows - n * SUBLANES]], axis=0)
    return jnp.concatenate([v[n * SUBLANES:]] + edge, axis=0)


def _dwconv(v, w, pad_left, wrap):
    out = None
    for j in range(w.shape[0]):
        term = w[j:j + 1] * _token_shift(v, j - pad_left, wrap)
        out = term if out is None else out + term
    return out


def _gate_terms(xc, pre, bias, lam):
    w = xc.shape[1]
    nl = -lam
    softplus = jnp.maximum(nl, 0.0) + jnp.log1p(jnp.exp(-jnp.abs(nl)))
    k = -LRU_C * softplus
    r = _sigmoid(pre[:, :w] + bias[0:1])
    i = _sigmoid(pre[:, w:] + bias[1:2])
    a = jnp.exp(k * r)
    mult = jnp.sqrt(1.0 - a * a)
    return a, mult * i * xc


def _segment_scan(a_scr, u_scr, h_in, reverse, emit):
    rows, width = a_scr.shape
    slabs = rows // SUBLANES

    def row_of(j):
        jj = slabs - 1 - j if reverse else j
        return pl.multiple_of(jj * SUBLANES, SUBLANES)

    def local_step(j, hp):
        h, p = hp
        r = row_of(j)
        a = a_scr[pl.ds(r, SUBLANES), :]
        return a * h + u_scr[pl.ds(r, SUBLANES), :], a * p

    init = (jnp.zeros((SUBLANES, width), F32), jnp.ones((SUBLANES, width), F32))
    h_loc, p_loc = lax.fori_loop(0, slabs, local_step, init, unroll=8)

    order = range(SUBLANES - 1, -1, -1) if reverse else range(SUBLANES)
    enter = [None] * SUBLANES
    state = h_in
    for s in order:
        enter[s] = state
        state = h_loc[s:s + 1] + p_loc[s:s + 1] * state
    h0 = jnp.concatenate(enter, axis=0)

    def final_step(j, h):
        r = row_of(j)
        h = a_scr[pl.ds(r, SUBLANES), :] * h + u_scr[pl.ds(r, SUBLANES), :]
        emit(r, h)
        return h

    lax.fori_loop(0, slabs, final_step, h0, unroll=8)
    return state


def _gates_to_scratch(xc, wg_ref, bg_ref, lam_ref, a_scr, u_scr):
    xcb = xc.astype(BF16)
    bw = xc.shape[1] // LRU_BLOCKS
    for n in range(LRU_BLOCKS):
        cs = slice(n * bw, (n + 1) * bw)
        pre = jnp.dot(xcb[:, cs], wg_ref[n], preferred_element_type=F32)
        a, u = _gate_terms(xc[:, cs], pre, bg_ref[:, cs], lam_ref[:, cs])
        a_scr[:, cs] = a
        u_scr[:, cs] = u


def _pass1_kernel(x_ref, mod_ref, g_ref, w4_ref, cw_ref, cb_ref, wg_ref, bg_ref, lam_ref, h0_ref,
                  xc_ref, hf_ref, fin_ref, a_scr, u_scr, carry_scr, *, wrap):
    @pl.when(pl.program_id(1) == 0)
    def _():
        carry_scr[...] = h0_ref[0]

    mod = mod_ref[0]
    xn = _mod_norm(x_ref[0], g_ref[...], mod[0:1], mod[1:2]).astype(BF16)
    lru_in = jnp.dot(xn, w4_ref[...], preferred_element_type=F32)
    xc = _dwconv(lru_in, cw_ref[...], CONV_B_PAD_LEFT, wrap) + cb_ref[...]
    xc_ref[0] = xc
    _gates_to_scratch(xc, wg_ref, bg_ref, lam_ref, a_scr, u_scr)

    def emit(r, h):
        hf_ref[0, pl.ds(r, SUBLANES), :] = h

    state = _segment_scan(a_scr, u_scr, carry_scr[...], False, emit)
    carry_scr[...] = state
    fin_ref[0] = state


def _pass2_kernel(x_ref, xc_ref, hf_ref, mod_ref, g_ref, win_ref, c3_ref, wg_ref, bg_ref, lam_ref,
                  woa_ref, wob_ref, wo_ref, h0_ref, fg_ref,
                  out_ref, fin_ref, a_scr, u_scr, xn_scr, ya_scr, yb_scr, mg_scr, carry_scr,
                  *, wrap, final_norm):
    @pl.when(pl.program_id(1) == 0)
    def _():
        carry_scr[...] = h0_ref[0]

    width = xc_ref.shape[2]
    mod = mod_ref[0]
    xn_scr[...] = _mod_norm(x_ref[0], g_ref[...], mod[0:1], mod[1:2]).astype(BF16)

    _gates_to_scratch(xc_ref[0], wg_ref, bg_ref, lam_ref, a_scr, u_scr)

    def emit(r, h):
        u_scr[pl.ds(r, SUBLANES), :] = hf_ref[0, pl.ds(r, SUBLANES), :] + h

    state = _segment_scan(a_scr, u_scr, carry_scr[...], True, emit)
    carry_scr[...] = state
    fin_ref[0] = state

    def proj(chunk, cs):
        lo = chunk * width + cs.start
        return jnp.dot(xn_scr[...], win_ref[:, lo:lo + COL_BLOCK], preferred_element_type=F32)

    blocks = [slice(i * COL_BLOCK, (i + 1) * COL_BLOCK) for i in range(width // COL_BLOCK)]
    for cs in blocks:
        gv = proj(2, cs) * proj(0, cs)
        conv = _dwconv(gv, c3_ref[:, cs], CONV_A_PAD_LEFT, wrap)
        ya_scr[:, cs] = (proj(1, cs) * conv * _silu(proj(3, cs))).astype(BF16)
        yb_scr[:, cs] = (u_scr[:, cs] * _silu(proj(5, cs))).astype(BF16)

    for cs in blocks:
        pa = jnp.dot(ya_scr[...], woa_ref[:, cs], preferred_element_type=F32)
        pb = jnp.dot(yb_scr[...], wob_ref[:, cs], preferred_element_type=F32)
        mg = _sigmoid(proj(6, cs)) * pa + _sigmoid(proj(7, cs)) * pb
        mg_scr[:, cs] = mg.astype(BF16)

    for cs in blocks:
        o = jnp.dot(mg_scr[...], wo_ref[:, cs], preferred_element_type=F32)
        out_ref[0, :, cs] = x_ref[0, :, cs] + mod[2:3, cs] * o

    if final_norm:
        y = out_ref[0]
        ms = jnp.mean(y * y, axis=-1, keepdims=True)
        out_ref[0] = y * lax.rsqrt(ms + RMS_EPS) * fg_ref[...]


def _resident(shape):
    zeros = (0,) * len(shape)
    return pl.BlockSpec(shape, lambda b, t: zeros, pipeline_mode=pl.Buffered(1))


def _pass1(x, mod, g, w4, cw, cb, wg, bg, lam, h0, *, tile, wrap):
    nb, seq, dm = x.shape
    width = w4.shape[1]
    nt = seq // tile
    tok = lambda w: pl.BlockSpec((1, tile, w), lambda b, t: (b, t, 0))
    per_b = lambda r, w: pl.BlockSpec((1, r, w), lambda b, t: (b, 0, 0))
    return pl.pallas_call(
        functools.partial(_pass1_kernel, wrap=wrap),
        grid=(nb, nt),
        in_specs=[tok(dm), per_b(3, dm), _resident(g.shape), _resident(w4.shape), _resident(cw.shape),
                  _resident(cb.shape), _resident(wg.shape), _resident(bg.shape), _resident(lam.shape),
                  per_b(1, width)],
        out_specs=[tok(width), tok(width), per_b(1, width)],
        out_shape=[jax.ShapeDtypeStruct((nb, seq, width), F32),
                   jax.ShapeDtypeStruct((nb, seq, width), F32),
                   jax.ShapeDtypeStruct((nb, 1, width), F32)],
        scratch_shapes=[pltpu.VMEM((tile, width), F32), pltpu.VMEM((tile, width), F32),
                        pltpu.VMEM((1, width), F32)],
        compiler_params=pltpu.CompilerParams(
            dimension_semantics=("arbitrary", "arbitrary"), vmem_limit_bytes=VMEM_LIMIT_BYTES),
        name="lru_pass1",
    )(x, mod, g, w4, cw, cb, wg, bg, lam, h0)


def _pass2(x, xc, hf, mod, g, win, c3, wg, bg, lam, woa, wob, wo, h0, fg, *, tile, wrap, final_norm):
    nb, seq, dm = x.shape
    width = xc.shape[2]
    nt = seq // tile
    tok = lambda w: pl.BlockSpec((1, tile, w), lambda b, t: (b, nt - 1 - t, 0))
    per_b = lambda r, w: pl.BlockSpec((1, r, w), lambda b, t: (b, 0, 0))
    return pl.pallas_call(
        functools.partial(_pass2_kernel, wrap=wrap, final_norm=final_norm),
        grid=(nb, nt),
        in_specs=[tok(dm), tok(width), tok(width), per_b(3, dm), _resident(g.shape), _resident(win.shape),
                  _resident(c3.shape), _resident(wg.shape), _resident(bg.shape), _resident(lam.shape),
                  _resident(woa.shape), _resident(wob.shape), _resident(wo.shape), per_b(1, width),
                  _resident(fg.shape)],
        out_specs=[tok(dm), per_b(1, width)],
        out_shape=[jax.ShapeDtypeStruct((nb, seq, dm), F32),
                   jax.ShapeDtypeStruct((nb, 1, width), F32)],
        scratch_shapes=[pltpu.VMEM((tile, width), F32), pltpu.VMEM((tile, width), F32),
                        pltpu.VMEM((tile, dm), BF16), pltpu.VMEM((tile, width), BF16),
                        pltpu.VMEM((tile, width), BF16), pltpu.VMEM((tile, dm), BF16),
                        pltpu.VMEM((1, width), F32)],
        compiler_params=pltpu.CompilerParams(
            dimension_semantics=("arbitrary", "arbitrary"), vmem_limit_bytes=VMEM_LIMIT_BYTES),
        name="lru_pass2",
    )(x, xc, hf, mod, g, win, c3, wg, bg, lam, woa, wob, wo, h0, fg)


def _ada_kernel(c_ref, w_ref, b_ref, o_ref):
    c = c_ref[...]
    o_ref[0] = jnp.dot(_silu(c), w_ref[0], preferred_element_type=F32,
                       precision=lax.Precision.HIGHEST) + b_ref[0]


def _ada_mods(cond, w_ada, b_ada):
    depth, dm, n = w_ada.shape
    nblk = n // dm
    return pl.pallas_call(
        _ada_kernel,
        grid=(depth, nblk),
        in_specs=[pl.BlockSpec(cond.shape, lambda l, j: (0, 0)),
                  pl.BlockSpec((1, dm, dm), lambda l, j: (l, 0, j)),
                  pl.BlockSpec((1, 1, dm), lambda l, j: (l, 0, j))],
        out_specs=pl.BlockSpec((1, cond.shape[0], dm), lambda l, j: (l, 0, j)),
        out_shape=jax.ShapeDtypeStruct((depth, cond.shape[0], n), F32),
        compiler_params=pltpu.CompilerParams(dimension_semantics=("arbitrary", "arbitrary")),
        name="ada_mods",
    )(cond, w_ada, b_ada.reshape(depth, 1, n))


def _to_segment_layout(v, tile):
    nb, seq, dm = v.shape
    v = v.reshape(nb, seq // tile, SUBLANES, tile // SUBLANES, dm)
    return v.transpose(0, 1, 3, 2, 4).reshape(nb, seq, dm)


def _from_segment_layout(v, tile):
    nb, seq, dm = v.shape
    v = v.reshape(nb, seq // tile, tile // SUBLANES, SUBLANES, dm)
    return v.transpose(0, 1, 3, 2, 4).reshape(nb, seq, dm)


def _gate_weights(wr, wi):
    return jnp.concatenate([wr, wi], axis=-1).astype(BF16)


def kernel(x, c, ctx, c_ctx, w_ada, b_ada, norm_g, w_in, conv3_w, conv4_w, conv4_b, lru_wr, lru_br,
           lru_wi, lru_bi, lru_lambda, w_out_a, w_out_b, w_o, final_g):
    nb, seq, dm = x.shape
    ctx_len = ctx.shape[1]
    depth = w_ada.shape[0]
    width = conv4_w.shape[2]
    assert seq % LATENT_TILE == 0 and ctx_len % SUBLANES == 0 and nb + 1 <= SUBLANES
    assert w_in.shape[2] == N_PROJ * width and width == dm and width % COL_BLOCK == 0

    cond = jnp.zeros((SUBLANES, dm), F32).at[:nb].set(c).at[nb].set(c_ctx)
    mods = _ada_mods(cond, w_ada, b_ada).reshape(depth, SUBLANES, 3, dm)

    xs = _to_segment_layout(x, LATENT_TILE)
    cs = _to_segment_layout(ctx, ctx_len)
    zero_state = jnp.zeros((nb, 1, width), F32)
    fg = final_g.reshape(1, dm)

    for l in range(depth):
        last = l == depth - 1
        mod_lat = mods[l, :nb]
        mod_ctx = jnp.broadcast_to(mods[l, nb], (nb, 3, dm))
        g = norm_g[l].reshape(1, dm)
        win = w_in[l].astype(BF16)
        w4 = win[:, 4 * width:5 * width]
        cb = conv4_b[l].reshape(1, width)
        woa, wob, wo = w_out_a[l].astype(BF16), w_out_b[l].astype(BF16), w_o[l].astype(BF16)
        fwd = (_gate_weights(lru_wr[l, 0], lru_wi[l, 0]),
               jnp.stack([lru_br[l, 0], lru_bi[l, 0]]), lru_lambda[l, 0].reshape(1, width))
        bwd = (_gate_weights(lru_wr[l, 1], lru_wi[l, 1]),
               jnp.stack([lru_br[l, 1], lru_bi[l, 1]]), lru_lambda[l, 1].reshape(1, width))

        p1 = functools.partial(_pass1, g=g, w4=w4, cw=conv4_w[l], cb=cb, wg=fwd[0], bg=fwd[1], lam=fwd[2])
        p2 = functools.partial(_pass2, g=g, win=win, c3=conv3_w[l], wg=bwd[0], bg=bwd[1], lam=bwd[2],
                               woa=woa, wob=wob, wo=wo, fg=fg)

        xc_c, hf_c, fin_f = p1(cs, mod_ctx, h0=zero_state, tile=ctx_len, wrap=True)
        cs_new, fin_b = p2(cs, xc_c, hf_c, mod_ctx, h0=zero_state, tile=ctx_len, wrap=True, final_norm=False)
        xc, hf, _ = p1(xs, mod_lat, h0=fin_f, tile=LATENT_TILE, wrap=False)
        xs, _ = p2(xs, xc, hf, mod_lat, h0=fin_b, tile=LATENT_TILE, wrap=False, final_norm=last)
        cs = cs_new

    return _from_segment_layout(xs, LATENT_TILE)
```

```python
import functools
import math

import jax
import jax.numpy as jnp
from jax import lax
from jax.experimental import pallas as pl
from jax.experimental.pallas import tpu as pltpu

GRID_W = 64
LRU_BLOCKS = 8
LRU_C = 8.0
RMS_EPS = 1e-6
N_PROJ = 8
HALVED_CHUNKS = (3, 5, 6, 7)
CONV_B_PAD_LEFT = 2
CONV_A_PAD_LEFT = 1

SUBLANES = 8
LATENT_TILE = SUBLANES * GRID_W
COL_BLOCK = 256
VMEM_LIMIT_BYTES = 60 * 1024 * 1024
SQRT_FLOOR = 1e-30

F32 = jnp.float32
BF16 = jnp.bfloat16


def _silu_of_half(zh):
    return zh + zh * jnp.tanh(zh)


def _mod_norm(x, g, shift, scale):
    ms = jnp.mean(x * x, axis=-1, keepdims=True)
    return x * lax.rsqrt(ms + RMS_EPS) * (g * (1.0 + scale)) + shift


def _token_shift(v, d, wrap):
    n = abs(d)
    if n == 0:
        return v
    rows, width = v.shape
    slabs = rows // SUBLANES
    if not wrap:
        edge = [jnp.zeros((n * SUBLANES, width), v.dtype)]
    else:
        sub = lax.broadcasted_iota(jnp.int32, (SUBLANES, width), 0)
        edge = []
        for i in range(n):
            if d < 0:
                src = (slabs - n + i) * SUBLANES
                slab = pltpu.roll(v[src:src + SUBLANES], 1, 0)
                edge.append(jnp.where(sub >= 1, slab, 0.0))
            else:
                src = i * SUBLANES
                slab = pltpu.roll(v[src:src + SUBLANES], SUBLANES - 1, 0)
                edge.append(jnp.where(sub <= SUBLANES - 2, slab, 0.0))
    if d < 0:
        return jnp.concatenate(edge + [v[:rows - n * SUBLANES]], axis=0)
    return jnp.concatenate([v[n * SUBLANES:]] + edge, axis=0)


def _dwconv(v, w, pad_left, wrap):
    out = None
    for j in range(w.shape[0]):
        term = w[j:j + 1] * _token_shift(v, j - pad_left, wrap)
        out = term if out is None else out + term
    return out


def _gate_terms(xc_half, pre_half, bias, lam):
    w = xc_half.shape[1]
    nl = -lam
    softplus = jnp.maximum(nl, 0.0) + jnp.log1p(jnp.exp(-jnp.abs(nl)))
    kh = (-0.5 * LRU_C * math.log2(math.e)) * softplus
    tr = jnp.tanh(pre_half[:, :w] + 0.5 * bias[0:1])
    ti = jnp.tanh(pre_half[:, w:] + 0.5 * bias[1:2])
    a = jnp.exp2(kh + kh * tr)
    y = 1.0 - a * a
    mult = y * lax.rsqrt(jnp.maximum(y, SQRT_FLOOR))
    return a, mult * (1.0 + ti) * xc_half


def _segment_scan(a_scr, u_scr, h_in, reverse, emit):
    rows, width = a_scr.shape
    slabs = rows // SUBLANES
    order = range(slabs - 1, -1, -1) if reverse else range(slabs)

    def slab(ref, j):
        return ref[j * SUBLANES:(j + 1) * SUBLANES, :]

    h_loc = jnp.zeros((SUBLANES, width), F32)
    p_loc = jnp.ones((SUBLANES, width), F32)
    for j in order:
        a = slab(a_scr, j)
        h_loc = a * h_loc + slab(u_scr, j)
        p_loc = a * p_loc

    seg_order = range(SUBLANES - 1, -1, -1) if reverse else range(SUBLANES)
    enter = [None] * SUBLANES
    state = h_in
    for s in seg_order:
        enter[s] = state
        state = h_loc[s:s + 1] + p_loc[s:s + 1] * state
    if emit is None:
        return state

    h = jnp.concatenate(enter, axis=0)
    for j in order:
        h = slab(a_scr, j) * h + slab(u_scr, j)
        emit(j * SUBLANES, h)
    return state


def _gates_to_scratch(xc, wg_ref, bg_ref, lam_ref, a_scr, u_scr):
    xcb = xc.astype(BF16)
    xch = 0.5 * xc
    bw = xc.shape[1] // LRU_BLOCKS
    for n in range(LRU_BLOCKS):
        cs = slice(n * bw, (n + 1) * bw)
        pre = jnp.dot(xcb[:, cs], wg_ref[n], preferred_element_type=F32)
        a, u = _gate_terms(xch[:, cs], pre, bg_ref[:, cs], lam_ref[:, cs])
        a_scr[:, cs] = a
        u_scr[:, cs] = u


def _pass1_kernel(x_ref, mod_ref, g_ref, w4_ref, cw_ref, cb_ref, wg_ref, bg_ref, lam_ref, h0_ref,
                  xc_ref, hf_ref, fin_ref, a_scr, u_scr, carry_scr, *, wrap):
    @pl.when(pl.program_id(1) == 0)
    def _():
        carry_scr[...] = h0_ref[0]

    mod = mod_ref[0]
    xn = _mod_norm(x_ref[0], g_ref[...], mod[0:1], mod[1:2]).astype(BF16)
    lru_in = jnp.dot(xn, w4_ref[...], preferred_element_type=F32)
    xc = _dwconv(lru_in, cw_ref[...], CONV_B_PAD_LEFT, wrap) + cb_ref[...]
    xc_ref[0] = xc
    _gates_to_scratch(xc, wg_ref, bg_ref, lam_ref, a_scr, u_scr)

    def emit(r, h):
        hf_ref[0, r:r + SUBLANES, :] = h

    state = _segment_scan(a_scr, u_scr, carry_scr[...], False, emit)
    carry_scr[...] = state
    fin_ref[0] = state


def _state_kernel(xc_ref, wg_ref, bg_ref, lam_ref, h0_ref, fin_ref, a_scr, u_scr, carry_scr):
    @pl.when(pl.program_id(1) == 0)
    def _():
        carry_scr[...] = h0_ref[0]

    _gates_to_scratch(xc_ref[0], wg_ref, bg_ref, lam_ref, a_scr, u_scr)
    state = _segment_scan(a_scr, u_scr, carry_scr[...], True, None)
    carry_scr[...] = state
    fin_ref[0] = state


def _pass2_kernel(x_ref, xc_ref, hf_ref, mod_ref, g_ref, win_ref, c3_ref, wg_ref, bg_ref, lam_ref,
                  woa_ref, wob_ref, wo_ref, h0_ref, fg_ref,
                  out_ref, fin_ref, a_scr, u_scr, xn_scr, ya_scr, yb_scr, mg_scr, carry_scr,
                  *, wrap, final_norm):
    @pl.when(pl.program_id(1) == 0)
    def _():
        carry_scr[...] = h0_ref[0]

    width = xc_ref.shape[2]
    mod = mod_ref[0]
    xn_scr[...] = _mod_norm(x_ref[0], g_ref[...], mod[0:1], mod[1:2]).astype(BF16)

    _gates_to_scratch(xc_ref[0], wg_ref, bg_ref, lam_ref, a_scr, u_scr)

    def emit(r, h):
        u_scr[r:r + SUBLANES, :] = hf_ref[0, r:r + SUBLANES, :] + h

    state = _segment_scan(a_scr, u_scr, carry_scr[...], True, emit)
    carry_scr[...] = state
    fin_ref[0] = state

    def proj(chunk, cs):
        lo = chunk * width + cs.start
        return jnp.dot(xn_scr[...], win_ref[:, lo:lo + COL_BLOCK], preferred_element_type=F32)

    blocks = [slice(i * COL_BLOCK, (i + 1) * COL_BLOCK) for i in range(width // COL_BLOCK)]
    for cs in blocks:
        gv = proj(2, cs) * proj(0, cs)
        conv = _dwconv(gv, c3_ref[:, cs], CONV_A_PAD_LEFT, wrap)
        ya_scr[:, cs] = (proj(1, cs) * conv * _silu_of_half(proj(3, cs))).astype(BF16)
        yb_scr[:, cs] = (u_scr[:, cs] * _silu_of_half(proj(5, cs))).astype(BF16)

    for cs in blocks:
        pa = jnp.dot(ya_scr[...], woa_ref[:, cs], preferred_element_type=F32)
        pb = jnp.dot(yb_scr[...], wob_ref[:, cs], preferred_element_type=F32)
        mg = (1.0 + jnp.tanh(proj(6, cs))) * pa + (1.0 + jnp.tanh(proj(7, cs))) * pb
        mg_scr[:, cs] = mg.astype(BF16)

    for cs in blocks:
        o = jnp.dot(mg_scr[...], wo_ref[:, cs], preferred_element_type=F32)
        out_ref[0, :, cs] = x_ref[0, :, cs] + mod[2:3, cs] * o

    if final_norm:
        y = out_ref[0]
        ms = jnp.mean(y * y, axis=-1, keepdims=True)
        out_ref[0] = y * lax.rsqrt(ms + RMS_EPS) * fg_ref[...]


def _resident(shape):
    zeros = (0,) * len(shape)
    return pl.BlockSpec(shape, lambda b, t: zeros, pipeline_mode=pl.Buffered(1))


def _per_batch(rows, width):
    return pl.BlockSpec((1, rows, width), lambda b, t: (b, 0, 0))


_SEQUENTIAL_GRID = pltpu.CompilerParams(
    dimension_semantics=("arbitrary", "arbitrary"), vmem_limit_bytes=VMEM_LIMIT_BYTES)


def _pass1(x, mod, g, w4, cw, cb, wg, bg, lam, h0, *, tile, wrap):
    nb, seq, dm = x.shape
    width = w4.shape[1]
    nt = seq // tile
    tok = lambda w: pl.BlockSpec((1, tile, w), lambda b, t: (b, t, 0))
    return pl.pallas_call(
        functools.partial(_pass1_kernel, wrap=wrap),
        grid=(nb, nt),
        in_specs=[tok(dm), _per_batch(3, dm), _resident(g.shape), _resident(w4.shape),
                  _resident(cw.shape), _resident(cb.shape), _resident(wg.shape), _resident(bg.shape),
                  _resident(lam.shape), _per_batch(1, width)],
        out_specs=[tok(width), tok(width), _per_batch(1, width)],
        out_shape=[jax.ShapeDtypeStruct((nb, seq, width), F32),
                   jax.ShapeDtypeStruct((nb, seq, width), F32),
                   jax.ShapeDtypeStruct((nb, 1, width), F32)],
        scratch_shapes=[pltpu.VMEM((tile, width), F32), pltpu.VMEM((tile, width), F32),
                        pltpu.VMEM((1, width), F32)],
        compiler_params=_SEQUENTIAL_GRID,
        name="lru_pass1",
    )(x, mod, g, w4, cw, cb, wg, bg, lam, h0)


def _final_state(xc, wg, bg, lam, h0, *, tile):
    nb, seq, width = xc.shape
    nt = seq // tile
    return pl.pallas_call(
        _state_kernel,
        grid=(nb, nt),
        in_specs=[pl.BlockSpec((1, tile, width), lambda b, t: (b, nt - 1 - t, 0)), _resident(wg.shape),
                  _resident(bg.shape), _resident(lam.shape), _per_batch(1, width)],
        out_specs=_per_batch(1, width),
        out_shape=jax.ShapeDtypeStruct((nb, 1, width), F32),
        scratch_shapes=[pltpu.VMEM((tile, width), F32), pltpu.VMEM((tile, width), F32),
                        pltpu.VMEM((1, width), F32)],
        compiler_params=_SEQUENTIAL_GRID,
        name="lru_state",
    )(xc, wg, bg, lam, h0)


def _pass2(x, xc, hf, mod, g, win, c3, wg, bg, lam, woa, wob, wo, h0, fg, *, tile, wrap, final_norm):
    nb, seq, dm = x.shape
    width = xc.shape[2]
    nt = seq // tile
    tok = lambda w: pl.BlockSpec((1, tile, w), lambda b, t: (b, nt - 1 - t, 0))
    return pl.pallas_call(
        functools.partial(_pass2_kernel, wrap=wrap, final_norm=final_norm),
        grid=(nb, nt),
        in_specs=[tok(dm), tok(width), tok(width), _per_batch(3, dm), _resident(g.shape),
                  _resident(win.shape), _resident(c3.shape), _resident(wg.shape), _resident(bg.shape),
                  _resident(lam.shape), _resident(woa.shape), _resident(wob.shape), _resident(wo.shape),
                  _per_batch(1, width), _resident(fg.shape)],
        out_specs=[tok(dm), _per_batch(1, width)],
        out_shape=[jax.ShapeDtypeStruct((nb, seq, dm), F32),
                   jax.ShapeDtypeStruct((nb, 1, width), F32)],
        scratch_shapes=[pltpu.VMEM((tile, width), F32), pltpu.VMEM((tile, width), F32),
                        pltpu.VMEM((tile, dm), BF16), pltpu.VMEM((tile, width), BF16),
                        pltpu.VMEM((tile, width), BF16), pltpu.VMEM((tile, dm), BF16),
                        pltpu.VMEM((1, width), F32)],
        compiler_params=_SEQUENTIAL_GRID,
        name="lru_pass2",
    )(x, xc, hf, mod, g, win, c3, wg, bg, lam, woa, wob, wo, h0, fg)


def _ada_kernel(c_ref, w_ref, b_ref, o_ref):
    c = c_ref[...]
    s = (c * (0.5 + 0.5 * jnp.tanh(0.5 * c))).astype(BF16)
    o_ref[0] = jnp.dot(s, w_ref[0].astype(BF16), preferred_element_type=F32) + b_ref[0]


def _ada_mods(cond, w_ada, b_ada):
    depth, dm, n = w_ada.shape
    nblk = n // dm
    return pl.pallas_call(
        _ada_kernel,
        grid=(depth, nblk),
        in_specs=[pl.BlockSpec(cond.shape, lambda l, j: (0, 0)),
                  pl.BlockSpec((1, dm, dm), lambda l, j: (l, 0, j)),
                  pl.BlockSpec((1, 1, dm), lambda l, j: (l, 0, j))],
        out_specs=pl.BlockSpec((1, cond.shape[0], dm), lambda l, j: (l, 0, j)),
        out_shape=jax.ShapeDtypeStruct((depth, cond.shape[0], n), F32),
        compiler_params=pltpu.CompilerParams(dimension_semantics=("arbitrary", "arbitrary")),
        name="ada_mods",
    )(cond, w_ada, b_ada.reshape(depth, 1, n))


def _to_segment_layout(v, tile):
    nb, seq, dm = v.shape
    v = v.reshape(nb, seq // tile, SUBLANES, tile // SUBLANES, dm)
    return v.transpose(0, 1, 3, 2, 4).reshape(nb, seq, dm)


def _from_segment_layout(v, tile):
    nb, seq, dm = v.shape
    v = v.reshape(nb, seq // tile, tile // SUBLANES, SUBLANES, dm)
    return v.transpose(0, 1, 3, 2, 4).reshape(nb, seq, dm)


def _gate_weights(wr, wi):
    return (0.5 * jnp.concatenate([wr, wi], axis=-1)).astype(BF16)


def kernel(x, c, ctx, c_ctx, w_ada, b_ada, norm_g, w_in, conv3_w, conv4_w, conv4_b, lru_wr, lru_br,
           lru_wi, lru_bi, lru_lambda, w_out_a, w_out_b, w_o, final_g):
    nb, seq, dm = x.shape
    ctx_len = ctx.shape[1]
    depth = w_ada.shape[0]
    width = conv4_w.shape[2]
    assert seq % LATENT_TILE == 0 and ctx_len % SUBLANES == 0 and nb + 1 <= SUBLANES
    assert w_in.shape[2] == N_PROJ * width and width == dm and width % COL_BLOCK == 0

    cond = jnp.zeros((SUBLANES, dm), F32).at[:nb].set(c).at[nb].set(c_ctx)
    mods = _ada_mods(cond, w_ada, b_ada).reshape(depth, SUBLANES, 3, dm)

    xs = _to_segment_layout(x, LATENT_TILE)
    cs = _to_segment_layout(ctx, ctx_len)
    zero_state = jnp.zeros((nb, 1, width), F32)
    fg = final_g.reshape(1, dm)
    chunk_scale = jnp.ones((N_PROJ, 1), F32).at[jnp.array(HALVED_CHUNKS)].set(0.5)
    col_scale = jnp.broadcast_to(chunk_scale, (N_PROJ, width)).reshape(1, N_PROJ * width)

    for l in range(depth):
        last = l == depth - 1
        mod_lat = mods[l, :nb]
        mod_ctx = jnp.broadcast_to(mods[l, nb], (nb, 3, dm))
        g = norm_g[l].reshape(1, dm)
        win = (w_in[l] * col_scale).astype(BF16)
        w4 = win[:, 4 * width:5 * width]
        cb = conv4_b[l].reshape(1, width)
        woa, wob, wo = w_out_a[l].astype(BF16), w_out_b[l].astype(BF16), (0.5 * w_o[l]).astype(BF16)
        fwd = (_gate_weights(lru_wr[l, 0], lru_wi[l, 0]),
               jnp.stack([lru_br[l, 0], lru_bi[l, 0]]), lru_lambda[l, 0].reshape(1, width))
        bwd = (_gate_weights(lru_wr[l, 1], lru_wi[l, 1]),
               jnp.stack([lru_br[l, 1], lru_bi[l, 1]]), lru_lambda[l, 1].reshape(1, width))

        p1 = functools.partial(_pass1, g=g, w4=w4, cw=conv4_w[l], cb=cb, wg=fwd[0], bg=fwd[1], lam=fwd[2])
        p2 = functools.partial(_pass2, g=g, win=win, c3=conv3_w[l], wg=bwd[0], bg=bwd[1], lam=bwd[2],
                               woa=woa, wob=wob, wo=wo, fg=fg)

        xc_c, hf_c, fin_f = p1(cs, mod_ctx, h0=zero_state, tile=ctx_len, wrap=True)
        if last:
            fin_b = _final_state(xc_c, *bwd, zero_state, tile=ctx_len)
        else:
            cs, fin_b = p2(cs, xc_c, hf_c, mod_ctx, h0=zero_state, tile=ctx_len, wrap=True,
                           final_norm=False)
        xc, hf, _ = p1(xs, mod_lat, h0=fin_f, tile=LATENT_TILE, wrap=False)
        xs, _ = p2(xs, xc, hf, mod_lat, h0=fin_b, tile=LATENT_TILE, wrap=False, final_norm=last)

    return _from_segment_layout(xs, LATENT_TILE)
```

```python
import functools
import math

import jax
import jax.numpy as jnp
from jax import lax
from jax.experimental import pallas as pl
from jax.experimental.pallas import tpu as pltpu

GRID_W = 64
LRU_BLOCKS = 8
LRU_C = 8.0
RMS_EPS = 1e-6
N_PROJ = 8
V, G_B, G_C, Z_A, LRU_IN, Z_B, M_A, M_B = range(N_PROJ)
HALVED_CHUNKS = (Z_A, Z_B, M_A, M_B)
CONV_B_PAD_LEFT = 2
CONV_A_PAD_LEFT = 1

SUBLANES = 8
LATENT_TILE = SUBLANES * GRID_W
COL_BLOCK = 256
ADA_K_BLOCK = 256
VMEM_LIMIT_BYTES = 60 * 1024 * 1024
SQRT_FLOOR = 1e-30

F32 = jnp.float32
BF16 = jnp.bfloat16


def _silu_of_half(zh):
    return zh + zh * jnp.tanh(zh)


def _mod_norm(x, g, shift, scale):
    ms = jnp.mean(x * x, axis=-1, keepdims=True)
    return x * lax.rsqrt(ms + RMS_EPS) * (g * (1.0 + scale)) + shift


def _token_shift(v, d, wrap):
    n = abs(d)
    if n == 0:
        return v
    rows, width = v.shape
    slabs = rows // SUBLANES
    if not wrap:
        edge = [jnp.zeros((n * SUBLANES, width), v.dtype)]
    else:
        sub = lax.broadcasted_iota(jnp.int32, (SUBLANES, width), 0)
        edge = []
        for i in range(n):
            if d < 0:
                src = (slabs - n + i) * SUBLANES
                slab = pltpu.roll(v[src:src + SUBLANES], 1, 0)
                edge.append(jnp.where(sub >= 1, slab, 0.0))
            else:
                src = i * SUBLANES
                slab = pltpu.roll(v[src:src + SUBLANES], SUBLANES - 1, 0)
                edge.append(jnp.where(sub <= SUBLANES - 2, slab, 0.0))
    if d < 0:
        return jnp.concatenate(edge + [v[:rows - n * SUBLANES]], axis=0)
    return jnp.concatenate([v[n * SUBLANES:]] + edge, axis=0)


def _dwconv(v, w, pad_left, wrap):
    out = None
    for j in range(w.shape[0]):
        term = w[j:j + 1] * _token_shift(v, j - pad_left, wrap)
        out = term if out is None else out + term
    return out


def _gate_terms(xc_half, pre_half, bias, lam):
    w = xc_half.shape[1]
    nl = -lam
    softplus = jnp.maximum(nl, 0.0) + jnp.log1p(jnp.exp(-jnp.abs(nl)))
    kh = (-0.5 * LRU_C * math.log2(math.e)) * softplus
    tr = jnp.tanh(pre_half[:, :w] + 0.5 * bias[0:1])
    ti = jnp.tanh(pre_half[:, w:] + 0.5 * bias[1:2])
    a = jnp.exp2(kh + kh * tr)
    y = 1.0 - a * a
    mult = y * lax.rsqrt(jnp.maximum(y, SQRT_FLOOR))
    return a, mult * (1.0 + ti) * xc_half


def _gate_dot(n, xcb_ref, wg_ref, pre_scr):
    bw = xcb_ref.shape[2] // LRU_BLOCKS
    pre_scr[:, 2 * n * bw:2 * (n + 1) * bw] = jnp.dot(
        xcb_ref[0, :, n * bw:(n + 1) * bw], wg_ref[n], preferred_element_type=F32)


def _gate_block(n, xch_ref, pre_scr, bg_ref, lam_ref, a_scr, u_scr):
    bw = xch_ref.shape[2] // LRU_BLOCKS
    cs = slice(n * bw, (n + 1) * bw)
    a, u = _gate_terms(xch_ref[0, :, cs], pre_scr[:, 2 * n * bw:2 * (n + 1) * bw], bg_ref[:, cs],
                       lam_ref[:, cs])
    a_scr[:, cs] = a
    u_scr[:, cs] = u


def _segment_scan(a_scr, u_scr, h_in, reverse, emit, fillers=()):
    rows, width = a_scr.shape
    slabs = rows // SUBLANES
    order = list(range(slabs - 1, -1, -1) if reverse else range(slabs))
    sweeps = 1 if emit is None else 2
    fillers = list(fillers)
    period = max(1, (sweeps * slabs) // max(1, len(fillers)))
    done = 0

    def tick():
        nonlocal done
        done += 1
        if done % period == 0 and fillers:
            fillers.pop(0)()

    def slab(ref, j):
        return ref[j * SUBLANES:(j + 1) * SUBLANES, :]

    h_loc = jnp.zeros((SUBLANES, width), F32)
    p_loc = jnp.ones((SUBLANES, width), F32)
    for j in order:
        a = slab(a_scr, j)
        h_loc = a * h_loc + slab(u_scr, j)
        p_loc = a * p_loc
        tick()

    seg_order = range(SUBLANES - 1, -1, -1) if reverse else range(SUBLANES)
    enter = [None] * SUBLANES
    state = h_in
    for s in seg_order:
        enter[s] = state
        state = h_loc[s:s + 1] + p_loc[s:s + 1] * state

    if emit is not None:
        h = jnp.concatenate(enter, axis=0)
        for j in order:
            h = slab(a_scr, j) * h + slab(u_scr, j)
            emit(j * SUBLANES, h)
            tick()
    while fillers:
        fillers.pop(0)()
    return state


def _col_blocks(width):
    return [slice(i * COL_BLOCK, (i + 1) * COL_BLOCK) for i in range(width // COL_BLOCK)]


def _pass1_kernel(x_ref, mod_ref, g_ref, win_ref, c3_ref, cw_ref, cb_ref, wg_ref, bg_ref, lam_ref,
                  woa_ref, h0_ref, xn_ref, xch_ref, xcb_ref, hf_ref, fin_ref, *rest, wrap, branch_a):
    if branch_a:
        pa_ref, a_scr, u_scr, pre_scr, ya_scr, carry_scr = rest
    else:
        a_scr, u_scr, pre_scr, carry_scr = rest

    @pl.when(pl.program_id(1) == 0)
    def _():
        carry_scr[...] = h0_ref[0]

    width = xch_ref.shape[2]
    mod = mod_ref[0]
    xn_ref[0] = _mod_norm(x_ref[0], g_ref[...], mod[0:1], mod[1:2]).astype(BF16)

    def proj(chunk, cs):
        lo = chunk * width + cs.start
        return jnp.dot(xn_ref[0], win_ref[:, lo:lo + COL_BLOCK], preferred_element_type=F32)

    def branch_a_block(cs):
        v, g_b, g_c, z_a = [proj(chunk, cs) for chunk in (V, G_B, G_C, Z_A)]
        conv = _dwconv(g_c * v, c3_ref[:, cs], CONV_A_PAD_LEFT, wrap)
        ya_scr[:, cs] = (g_b * conv * _silu_of_half(z_a)).astype(BF16)

    def gate_heads(lo, hi):
        for n in range(lo, hi):
            _gate_block(n, xch_ref, pre_scr, bg_ref, lam_ref, a_scr, u_scr)

    blocks = _col_blocks(width)
    assert len(blocks) == 4 or not branch_a
    lru_in = jnp.dot(xn_ref[0], win_ref[:, LRU_IN * width:(LRU_IN + 1) * width],
                     preferred_element_type=F32)
    if branch_a:
        branch_a_block(blocks[0])
    xch = _dwconv(lru_in, 0.5 * cw_ref[...], CONV_B_PAD_LEFT, wrap) + 0.5 * cb_ref[...]
    xch_ref[0] = xch
    xcb_ref[0] = xch.astype(BF16)
    for n in range(LRU_BLOCKS):
        _gate_dot(n, xcb_ref, wg_ref, pre_scr)
    if branch_a:
        branch_a_block(blocks[1])
    gate_heads(0, LRU_BLOCKS // 2)
    if branch_a:
        branch_a_block(blocks[2])
    gate_heads(LRU_BLOCKS // 2, LRU_BLOCKS)
    if branch_a:
        branch_a_block(blocks[3])

    def gated_branch_a(cs):
        def run():
            pa = jnp.dot(ya_scr[...], woa_ref[:, cs], preferred_element_type=F32)
            pa_ref[0, :, cs] = (1.0 + jnp.tanh(proj(M_A, cs))) * pa
        return run

    def emit(r, h):
        hf_ref[0, r:r + SUBLANES, :] = h

    fillers = [gated_branch_a(cs) for cs in blocks] if branch_a else ()
    state = _segment_scan(a_scr, u_scr, carry_scr[...], False, emit, fillers)
    carry_scr[...] = state
    fin_ref[0] = state


def _state_kernel(xch_ref, xcb_ref, wg_ref, bg_ref, lam_ref, h0_ref, fin_ref, a_scr, u_scr, pre_scr,
                  carry_scr):
    @pl.when(pl.program_id(1) == 0)
    def _():
        carry_scr[...] = h0_ref[0]

    for n in range(LRU_BLOCKS):
        _gate_dot(n, xcb_ref, wg_ref, pre_scr)
    for n in range(LRU_BLOCKS):
        _gate_block(n, xch_ref, pre_scr, bg_ref, lam_ref, a_scr, u_scr)
    state = _segment_scan(a_scr, u_scr, carry_scr[...], True, None)
    carry_scr[...] = state
    fin_ref[0] = state


def _pass2_kernel(x_ref, xn_ref, xch_ref, xcb_ref, hf_ref, pa_ref, mod_ref, wz_ref, wg_ref, bg_ref,
                  lam_ref, wob_ref, wo_ref, h0_ref, fg_ref,
                  out_ref, fin_ref, a_scr, u_scr, pre_scr, zb_scr, mb_scr, yb_scr, mg_scr, carry_scr,
                  *, final_norm):
    @pl.when(pl.program_id(1) == 0)
    def _():
        carry_scr[...] = h0_ref[0]

    width = xch_ref.shape[2]

    def proj(which, cs):
        lo = which * width + cs.start
        return jnp.dot(xn_ref[0], wz_ref[:, lo:lo + COL_BLOCK], preferred_element_type=F32)

    for n in range(LRU_BLOCKS):
        _gate_dot(n, xcb_ref, wg_ref, pre_scr)
    blocks = _col_blocks(width)
    heads_per_block = LRU_BLOCKS // len(blocks)
    for i, cs in enumerate(blocks):
        for n in range(i * heads_per_block, (i + 1) * heads_per_block):
            _gate_block(n, xch_ref, pre_scr, bg_ref, lam_ref, a_scr, u_scr)
        zb_scr[:, cs] = proj(0, cs)

    def merge_gate_b(cs):
        def run():
            mb_scr[:, cs] = proj(1, cs)
        return run

    def emit(r, h):
        u_scr[r:r + SUBLANES, :] = hf_ref[0, r:r + SUBLANES, :] + h

    state = _segment_scan(a_scr, u_scr, carry_scr[...], True, emit,
                          [merge_gate_b(cs) for cs in blocks])
    carry_scr[...] = state
    fin_ref[0] = state

    yb_scr[...] = (u_scr[...] * _silu_of_half(zb_scr[...])).astype(BF16)
    for cs in blocks:
        pb = jnp.dot(yb_scr[...], wob_ref[:, cs], preferred_element_type=F32)
        mg_scr[:, cs] = (pa_ref[0, :, cs] + (1.0 + jnp.tanh(mb_scr[:, cs])) * pb).astype(BF16)
    gate = mod_ref[0][2:3]
    for cs in blocks:
        o = jnp.dot(mg_scr[...], wo_ref[:, cs], preferred_element_type=F32)
        out_ref[0, :, cs] = x_ref[0, :, cs] + gate[:, cs] * o

    if final_norm:
        y = out_ref[0]
        ms = jnp.mean(y * y, axis=-1, keepdims=True)
        out_ref[0] = y * lax.rsqrt(ms + RMS_EPS) * fg_ref[...]


def _resident(shape):
    zeros = (0,) * len(shape)
    return pl.BlockSpec(shape, lambda b, t: zeros, pipeline_mode=pl.Buffered(1))


def _per_batch(rows, width):
    return pl.BlockSpec((1, rows, width), lambda b, t: (b, 0, 0))


_SEQUENTIAL_GRID = pltpu.CompilerParams(
    dimension_semantics=("arbitrary", "arbitrary"), vmem_limit_bytes=VMEM_LIMIT_BYTES)


def _scan_scratch(tile, width):
    return [pltpu.VMEM((tile, width), F32), pltpu.VMEM((tile, width), F32),
            pltpu.VMEM((tile, 2 * width), F32)]


def _pass1(x, mod, g, win, c3, cw, cb, wg, bg, lam, woa, h0, *, tile, wrap, branch_a):
    nb, seq, dm = x.shape
    width = cw.shape[1]
    nt = seq // tile
    tok = lambda w: pl.BlockSpec((1, tile, w), lambda b, t: (b, t, 0))
    tok_f32 = jax.ShapeDtypeStruct((nb, seq, width), F32)
    out_specs = [tok(dm), tok(width), tok(width), tok(width), _per_batch(1, width)]
    out_shape = [jax.ShapeDtypeStruct((nb, seq, dm), BF16), tok_f32,
                 jax.ShapeDtypeStruct((nb, seq, width), BF16), tok_f32,
                 jax.ShapeDtypeStruct((nb, 1, width), F32)]
    scratch = _scan_scratch(tile, width)
    if branch_a:
        out_specs.append(tok(width))
        out_shape.append(tok_f32)
        scratch.append(pltpu.VMEM((tile, width), BF16))
    scratch.append(pltpu.VMEM((1, width), F32))
    return pl.pallas_call(
        functools.partial(_pass1_kernel, wrap=wrap, branch_a=branch_a),
        grid=(nb, nt),
        in_specs=[tok(dm), _per_batch(3, dm), _resident(g.shape), _resident(win.shape),
                  _resident(c3.shape), _resident(cw.shape), _resident(cb.shape), _resident(wg.shape),
                  _resident(bg.shape), _resident(lam.shape), _resident(woa.shape),
                  _per_batch(1, width)],
        out_specs=out_specs,
        out_shape=out_shape,
        scratch_shapes=scratch,
        compiler_params=_SEQUENTIAL_GRID,
        name="lru_pass1",
    )(x, mod, g, win, c3, cw, cb, wg, bg, lam, woa, h0)


def _final_state(xch, xcb, wg, bg, lam, h0, *, tile):
    nb, seq, width = xch.shape
    nt = seq // tile
    tok = pl.BlockSpec((1, tile, width), lambda b, t: (b, nt - 1 - t, 0))
    return pl.pallas_call(
        _state_kernel,
        grid=(nb, nt),
        in_specs=[tok, tok, _resident(wg.shape), _resident(bg.shape), _resident(lam.shape),
                  _per_batch(1, width)],
        out_specs=_per_batch(1, width),
        out_shape=jax.ShapeDtypeStruct((nb, 1, width), F32),
        scratch_shapes=_scan_scratch(tile, width) + [pltpu.VMEM((1, width), F32)],
        compiler_params=_SEQUENTIAL_GRID,
        name="lru_state",
    )(xch, xcb, wg, bg, lam, h0)


def _pass2(x, xn, xch, xcb, hf, pa, mod, wz, wg, bg, lam, wob, wo, h0, fg, *, tile, final_norm):
    nb, seq, dm = x.shape
    width = xch.shape[2]
    nt = seq // tile
    tok = lambda w: pl.BlockSpec((1, tile, w), lambda b, t: (b, nt - 1 - t, 0))
    return pl.pallas_call(
        functools.partial(_pass2_kernel, final_norm=final_norm),
        grid=(nb, nt),
        in_specs=[tok(dm), tok(dm), tok(width), tok(width), tok(width), tok(width), _per_batch(3, dm),
                  _resident(wz.shape), _resident(wg.shape), _resident(bg.shape), _resident(lam.shape),
                  _resident(wob.shape), _resident(wo.shape), _per_batch(1, width), _resident(fg.shape)],
        out_specs=[tok(dm), _per_batch(1, width)],
        out_shape=[jax.ShapeDtypeStruct((nb, seq, dm), F32),
                   jax.ShapeDtypeStruct((nb, 1, width), F32)],
        scratch_shapes=_scan_scratch(tile, width) + [
            pltpu.VMEM((tile, width), F32), pltpu.VMEM((tile, width), F32),
            pltpu.VMEM((tile, width), BF16), pltpu.VMEM((tile, dm), BF16),
            pltpu.VMEM((1, width), F32)],
        compiler_params=_SEQUENTIAL_GRID,
        name="lru_pass2",
    )(x, xn, xch, xcb, hf, pa, mod, wz, wg, bg, lam, wob, wo, h0, fg)


def _ada_kernel(c_ref, w_ref, b_ref, o_ref):
    @pl.when(pl.program_id(1) == 0)
    def _():
        o_ref[0] = jnp.broadcast_to(b_ref[0], o_ref.shape[1:])

    c = c_ref[...]
    s = (c * (0.5 + 0.5 * jnp.tanh(0.5 * c))).astype(BF16)
    o_ref[0] += jnp.dot(s, w_ref[0].astype(BF16), preferred_element_type=F32)


def _ada_mods(cond, w_ada, b_ada):
    depth, dm, n = w_ada.shape
    rows = cond.shape[0]
    return pl.pallas_call(
        _ada_kernel,
        grid=(depth, dm // ADA_K_BLOCK),
        in_specs=[pl.BlockSpec((rows, ADA_K_BLOCK), lambda l, k: (0, k)),
                  pl.BlockSpec((1, ADA_K_BLOCK, n), lambda l, k: (l, k, 0)),
                  pl.BlockSpec((1, 1, n), lambda l, k: (l, 0, 0))],
        out_specs=pl.BlockSpec((1, rows, n), lambda l, k: (l, 0, 0)),
        out_shape=jax.ShapeDtypeStruct((depth, rows, n), F32),
        compiler_params=pltpu.CompilerParams(dimension_semantics=("arbitrary", "arbitrary")),
        name="ada_mods",
    )(cond, w_ada, b_ada.reshape(depth, 1, n))


def _to_segment_layout(v, tile):
    nb, seq, dm = v.shape
    v = v.reshape(nb, seq // tile, SUBLANES, tile // SUBLANES, dm)
    return v.transpose(0, 1, 3, 2, 4).reshape(nb, seq, dm)


def _from_segment_layout(v, tile):
    nb, seq, dm = v.shape
    v = v.reshape(nb, seq // tile, tile // SUBLANES, SUBLANES, dm)
    return v.transpose(0, 1, 3, 2, 4).reshape(nb, seq, dm)


def _gate_weights(wr, wi):
    return jnp.concatenate([wr, wi], axis=-1).astype(BF16)


def kernel(x, c, ctx, c_ctx, w_ada, b_ada, norm_g, w_in, conv3_w, conv4_w, conv4_b, lru_wr, lru_br,
           lru_wi, lru_bi, lru_lambda, w_out_a, w_out_b, w_o, final_g):
    nb, seq, dm = x.shape
    ctx_len = ctx.shape[1]
    depth = w_ada.shape[0]
    width = conv4_w.shape[2]
    assert seq % LATENT_TILE == 0 and ctx_len % SUBLANES == 0 and nb + 1 <= SUBLANES
    assert w_in.shape[2] == N_PROJ * width and width == dm and width % COL_BLOCK == 0
    assert LRU_BLOCKS % (width // COL_BLOCK) == 0 and dm % ADA_K_BLOCK == 0

    cond = jnp.zeros((SUBLANES, dm), F32).at[:nb].set(c).at[nb].set(c_ctx)
    mods = _ada_mods(cond, w_ada, b_ada).reshape(depth, SUBLANES, 3, dm)

    xs = _to_segment_layout(x, LATENT_TILE)
    cs = _to_segment_layout(ctx, ctx_len)
    zero_state = jnp.zeros((nb, 1, width), F32)
    fg = final_g.reshape(1, dm)
    chunk_scale = jnp.ones((N_PROJ, 1), F32).at[jnp.array(HALVED_CHUNKS)].set(0.5)
    col_scale = jnp.broadcast_to(chunk_scale, (N_PROJ, width)).reshape(1, N_PROJ * width)

    for l in range(depth):
        last = l == depth - 1
        mod_lat = mods[l, :nb]
        mod_ctx = jnp.broadcast_to(mods[l, nb], (nb, 3, dm))
        g = norm_g[l].reshape(1, dm)
        win = (w_in[l] * col_scale).astype(BF16)
        wz = jnp.concatenate([win[:, Z_B * width:(Z_B + 1) * width],
                              win[:, M_B * width:(M_B + 1) * width]], axis=1)
        cb = conv4_b[l].reshape(1, width)
        woa, wob, wo = w_out_a[l].astype(BF16), w_out_b[l].astype(BF16), (0.5 * w_o[l]).astype(BF16)
        fwd = (_gate_weights(lru_wr[l, 0], lru_wi[l, 0]),
               jnp.stack([lru_br[l, 0], lru_bi[l, 0]]), lru_lambda[l, 0].reshape(1, width))
        bwd = (_gate_weights(lru_wr[l, 1], lru_wi[l, 1]),
               jnp.stack([lru_br[l, 1], lru_bi[l, 1]]), lru_lambda[l, 1].reshape(1, width))

        p1 = functools.partial(_pass1, g=g, win=win, c3=conv3_w[l], cw=conv4_w[l], cb=cb,
                               wg=fwd[0], bg=fwd[1], lam=fwd[2], woa=woa)
        p2 = functools.partial(_pass2, wz=wz, wg=bwd[0], bg=bwd[1], lam=bwd[2],
                               wob=wob, wo=wo, fg=fg)

        if last:
            _, xch_c, xcb_c, _, fin_f = p1(cs, mod_ctx, h0=zero_state, tile=ctx_len, wrap=True,
                                           branch_a=False)
            fin_b = _final_state(xch_c, xcb_c, *bwd, zero_state, tile=ctx_len)
        else:
            xn_c, xch_c, xcb_c, hf_c, fin_f, pa_c = p1(cs, mod_ctx, h0=zero_state, tile=ctx_len,
                                                       wrap=True, branch_a=True)
            cs, fin_b = p2(cs, xn_c, xch_c, xcb_c, hf_c, pa_c, mod_ctx, h0=zero_state, tile=ctx_len,
                           final_norm=False)
        xn, xch, xcb, hf, _, pa = p1(xs, mod_lat, h0=fin_f, tile=LATENT_TILE, wrap=False,
                                     branch_a=True)
        xs, _ = p2(xs, xn, xch, xcb, hf, pa, mod_lat, h0=fin_b, tile=LATENT_TILE, final_norm=last)

    return _from_segment_layout(xs, LATENT_TILE)
```
